```python
import jax, jax.numpy as jnp
from jax import lax
import numpy as np

D_MODEL = 2048
BATCH = 2
SEQ = 8192
DEPTH = 1

D_MIX = D_MODEL
D_LRU = D_MIX // 2
D_CONV = D_MIX - D_LRU
LRU_HEADS = 16
LRU_HEAD_DIM = D_LRU // LRU_HEADS
LRU_CONV_WIDTH = 4
LRU_C = 8.0
CONF_CONV_WIDTH = 31
D_FF = 4 * D_MODEL
D_IN = 2 * D_LRU + 2 * D_CONV
EPS = 1e-6

kernel_name = "hymba_style_rglru_conformer_conv_hybrid"


def rms_norm(x, g):
    xf = x.astype(jnp.float32)
    y = xf * lax.rsqrt(jnp.mean(xf * xf, axis=-1, keepdims=True) + EPS)
    return (y * g.astype(jnp.float32)).astype(x.dtype)


def layer_norm(x, g, b):
    xf = x.astype(jnp.float32)
    mu = jnp.mean(xf, axis=-1, keepdims=True)
    xc = xf - mu
    var = jnp.mean(xc * xc, axis=-1, keepdims=True)
    y = xc * lax.rsqrt(var + EPS)
    return (y * g.astype(jnp.float32) + b.astype(jnp.float32)).astype(x.dtype)


def causal_depthwise_conv(x, w, b):
    k = w.shape[0]
    y = lax.conv_general_dilated(
        x, w[:, None, :].astype(x.dtype), window_strides=(1,), padding=[(k - 1, 0)],
        dimension_numbers=("NWC", "WIO", "NWC"), feature_group_count=x.shape[-1])
    return y + b.astype(x.dtype)


def _linear_recurrence_combine(left, right):
    a_l, b_l = left
    a_r, b_r = right
    return a_l * a_r, a_r * b_l + b_r


def rg_lru(xc, w_a, b_a, w_x, b_x, lam):
    bsz, s, c = xc.shape
    xh = xc.reshape(bsz, s, LRU_HEADS, LRU_HEAD_DIM)
    r = jax.nn.sigmoid(jnp.einsum("bshi,hij->bshj", xh, w_a).reshape(bsz, s, c) + b_a)
    i = jax.nn.sigmoid(jnp.einsum("bshi,hij->bshj", xh, w_x).reshape(bsz, s, c) + b_x)
    log_a = -LRU_C * r.astype(jnp.float32) * jax.nn.softplus(-lam.astype(jnp.float32))
    a = jnp.exp(log_a)
    mult = jnp.sqrt(-jnp.expm1(2.0 * log_a))
    u = mult * (i * xc).astype(jnp.float32)
    _, h = lax.associative_scan(_linear_recurrence_combine, (a, u), axis=1)
    return h.astype(xc.dtype)


def setup_inputs(seed: int = 0) -> dict:
    key = jax.random.key(seed)
    ks = jax.random.split(key, 24)
    f32 = jnp.float32

    def nrm(k, shape, scale):
        return jax.random.normal(k, shape, f32) * scale

    a_c = jax.random.uniform(ks[9], (DEPTH, D_LRU), f32, 0.9, 0.999)
    s = a_c ** (1.0 / LRU_C)
    lru_lambda = jnp.log(s) - jnp.log1p(-s)

    return {
        "x": nrm(ks[0], (BATCH, SEQ, D_MODEL), 1.0),
        "mix_norm_g": 1.0 + nrm(ks[1], (DEPTH, D_MODEL), 0.02),
        "w_in": nrm(ks[2], (DEPTH, D_MODEL, D_IN), D_MODEL ** -0.5),
        "lru_conv_w": nrm(ks[3], (DEPTH, LRU_CONV_WIDTH, D_LRU), LRU_CONV_WIDTH ** -0.5),
        "lru_conv_b": nrm(ks[4], (DEPTH, D_LRU), 0.01),
        "lru_gate_a_w": nrm(ks[5], (DEPTH, LRU_HEADS, LRU_HEAD_DIM, LRU_HEAD_DIM), LRU_HEAD_DIM ** -0.5),
        "lru_gate_a_b": nrm(ks[6], (DEPTH, D_LRU), 0.01),
        "lru_gate_x_w": nrm(ks[7], (DEPTH, LRU_HEADS, LRU_HEAD_DIM, LRU_HEAD_DIM), LRU_HEAD_DIM ** -0.5),
        "lru_gate_x_b": nrm(ks[8], (DEPTH, D_LRU), 0.01),
        "lru_lambda": lru_lambda,
        "conf_dw_w": nrm(ks[10], (DEPTH, CONF_CONV_WIDTH, D_CONV), CONF_CONV_WIDTH ** -0.5),
        "conf_dw_b": nrm(ks[11], (DEPTH, D_CONV), 0.01),
        "conf_ln_g": 1.0 + nrm(ks[12], (DEPTH, D_CONV), 0.02),
        "conf_ln_b": nrm(ks[13], (DEPTH, D_CONV), 0.01),
        "w_out": nrm(ks[14], (DEPTH, D_MIX, D_MODEL), D_MIX ** -0.5),
        "mlp_norm_g": 1.0 + nrm(ks[15], (DEPTH, D_MODEL), 0.02),
        "mlp_w1": nrm(ks[16], (DEPTH, D_MODEL, D_FF), D_MODEL ** -0.5),
        "mlp_w2": nrm(ks[17], (DEPTH, D_FF, D_MODEL), D_FF ** -0.5),
        "final_norm_g": 1.0 + nrm(ks[18], (D_MODEL,), 0.02),
    }


def reference(x, mix_norm_g, w_in, lru_conv_w, lru_conv_b, lru_gate_a_w, lru_gate_a_b,
              lru_gate_x_w, lru_gate_x_b, lru_lambda, conf_dw_w, conf_dw_b, conf_ln_g,
              conf_ln_b, w_out, mlp_norm_g, mlp_w1, mlp_w2, final_norm_g):
    h = x
    for l in range(DEPTH):
        xn = rms_norm(h, mix_norm_g[l])
        u = xn @ w_in[l]
        x_lru = u[..., :D_LRU]
        y_lru = u[..., D_LRU:2 * D_LRU]
        c_val = u[..., 2 * D_LRU:2 * D_LRU + D_CONV]
        c_gate = u[..., 2 * D_LRU + D_CONV:]

        xc = causal_depthwise_conv(x_lru, lru_conv_w[l], lru_conv_b[l])
        hr = rg_lru(xc, lru_gate_a_w[l], lru_gate_a_b[l], lru_gate_x_w[l], lru_gate_x_b[l],
                    lru_lambda[l])
        o_lru = hr * jax.nn.gelu(y_lru)

        g = c_val * jax.nn.sigmoid(c_gate)
        g = causal_depthwise_conv(g, conf_dw_w[l], conf_dw_b[l])
        g = layer_norm(g, conf_ln_g[l], conf_ln_b[l])
        o_conv = jax.nn.silu(g)

        o = jnp.concatenate([o_lru, o_conv], axis=-1) @ w_out[l]
        h = h + o

        z = rms_norm(h, mlp_norm_g[l]) @ mlp_w1[l]
        h = h + jnp.square(jax.nn.relu(z)) @ mlp_w2[l]
    return rms_norm(h, final_norm_g)
```

```python
import functools

import jax
import jax.numpy as jnp
from jax import lax
from jax.experimental import pallas as pl
from jax.experimental.pallas import tpu as pltpu

D_MODEL = 2048
D_LRU = 1024
D_CONV = 1024
LRU_HEADS = 16
LRU_HEAD_DIM = D_LRU // LRU_HEADS
LRU_CONV_WIDTH = 4
LRU_C = 8.0
CONF_CONV_WIDTH = 31
D_FF = 4 * D_MODEL
D_IN = 2 * D_LRU + 2 * D_CONV
EPS = 1e-6

SUBLANES = 8
GATE_GROUP = 256
N_GATE_GROUPS = D_LRU // GATE_GROUP
LRU_HALO = 8
CONF_HALO = 32
VMEM_LIMIT_BYTES = 56 * 1024 * 1024

F32 = jnp.float32
BF16 = jnp.bfloat16


def _sigmoid(x):
    return 1.0 / (1.0 + jnp.exp(-x))


def _rms_scale(x):
    return x * lax.rsqrt(jnp.mean(x * x, axis=-1, keepdims=True) + EPS)


def _inproj_kernel(x_ref, g_ref, w_ref, u_ref, xn_ref):
    @pl.when(pl.program_id(1) == 0)
    def _():
        xn_ref[...] = (_rms_scale(x_ref[...]) * g_ref[...]).astype(BF16)

    u_ref[...] = jnp.dot(xn_ref[...], w_ref[...], preferred_element_type=F32)


def _inproj(x2, g, w, tm, tn):
    n_tok = x2.shape[0]
    return pl.pallas_call(
        _inproj_kernel,
        grid=(n_tok // tm, D_IN // tn),
        in_specs=[
            pl.BlockSpec((tm, D_MODEL), lambda i, j: (i, 0)),
            pl.BlockSpec((1, D_MODEL), lambda i, j: (0, 0)),
            pl.BlockSpec((D_MODEL, tn), lambda i, j: (0, j)),
        ],
        out_specs=pl.BlockSpec((tm, tn), lambda i, j: (i, j)),
        out_shape=jax.ShapeDtypeStruct((n_tok, D_IN), F32),
        scratch_shapes=[pltpu.VMEM((tm, D_MODEL), BF16)],
        compiler_params=pltpu.CompilerParams(
            dimension_semantics=("arbitrary", "arbitrary"),
            vmem_limit_bytes=VMEM_LIMIT_BYTES),
        name="inproj",
    )(x2, g, w)


def _mixer_kernel(u_ref, x_ref, cw_ref, cb_ref, gwa_ref, gwx_ref, ba_ref, bx_ref, lam_ref,
                  dww_ref, dwb_ref, lng_ref, lnb_ref, wout_ref, h_ref,
                  xpad_ref, gpad_ref, a_ref, b_ref, hr_ref, carry_ref, *, ts):
    @pl.when(pl.program_id(1) == 0)
    def _():
        xpad_ref[0:LRU_HALO, :] = jnp.zeros((LRU_HALO, D_LRU), F32)
        gpad_ref[0:CONF_HALO, :] = jnp.zeros((CONF_HALO, D_CONV), F32)
        carry_ref[...] = jnp.zeros((SUBLANES, D_LRU), F32)

    x_lru = u_ref[:, 0:D_LRU]
    xpad_ref[LRU_HALO:LRU_HALO + ts, :] = x_lru
    xc = jnp.broadcast_to(cb_ref[...], (ts, D_LRU))
    for k in range(LRU_CONV_WIDTH):
        off = LRU_HALO - (LRU_CONV_WIDTH - 1) + k
        xc = xc + cw_ref[k:k + 1, :] * xpad_ref[off:off + ts, :]
    xpad_ref[0:LRU_HALO, :] = x_lru[ts - LRU_HALO:ts, :]

    xcb = xc.astype(BF16)
    za, zx = [], []
    for g in range(N_GATE_GROUPS):
        xg = xcb[:, g * GATE_GROUP:(g + 1) * GATE_GROUP]
        za.append(jnp.dot(xg, gwa_ref[g], preferred_element_type=F32))
        zx.append(jnp.dot(xg, gwx_ref[g], preferred_element_type=F32))
    r = _sigmoid(jnp.concatenate(za, axis=-1) + ba_ref[...])
    gate_i = _sigmoid(jnp.concatenate(zx, axis=-1) + bx_ref[...])

    neg_lam = -lam_ref[...]
    softplus = jnp.maximum(neg_lam, 0.0) + jnp.log1p(jnp.exp(-jnp.abs(neg_lam)))
    log_a = (-LRU_C * r) * softplus
    a = jnp.exp(log_a)
    mult = jnp.sqrt(-jnp.tanh(log_a) * (a * a + 1.0))
    a_ref[...] = a
    b_ref[...] = mult * (gate_i * xc)

    row = lax.broadcasted_iota(jnp.int32, (SUBLANES, D_LRU), 0)

    def scan_body(i, h_prev):
        r0 = pl.multiple_of(i * SUBLANES, SUBLANES)
        aa = a_ref[pl.ds(r0, SUBLANES), :]
        bb = b_ref[pl.ds(r0, SUBLANES), :]
        for d in (1, 2, 4):
            keep = row >= d
            a_s = jnp.where(keep, pltpu.roll(aa, d, 0), 1.0)
            b_s = jnp.where(keep, pltpu.roll(bb, d, 0), 0.0)
            bb = aa * b_s + bb
            aa = aa * a_s
        h = bb + aa * h_prev
        hr_ref[pl.ds(r0, SUBLANES), :] = h
        return jnp.broadcast_to(h[SUBLANES - 1:SUBLANES, :], (SUBLANES, D_LRU))

    carry_ref[...] = lax.fori_loop(0, ts // SUBLANES, scan_body, carry_ref[...])

    y_lru = u_ref[:, D_LRU:2 * D_LRU]
    o_lru = hr_ref[...] * jax.nn.gelu(y_lru)

    glu = u_ref[:, 2 * D_LRU:2 * D_LRU + D_CONV] * _sigmoid(u_ref[:, 2 * D_LRU + D_CONV:])
    gpad_ref[CONF_HALO:CONF_HALO + ts, :] = glu
    cv = jnp.broadcast_to(dwb_ref[...], (ts, D_CONV))
    for k in range(CONF_CONV_WIDTH):
        off = CONF_HALO - (CONF_CONV_WIDTH - 1) + k
        cv = cv + dww_ref[k:k + 1, :] * gpad_ref[off:off + ts, :]
    gpad_ref[0:CONF_HALO, :] = glu[ts - CONF_HALO:ts, :]

    mu = jnp.mean(cv, axis=-1, keepdims=True)
    cc = cv - mu
    var = jnp.mean(cc * cc, axis=-1, keepdims=True)
    ln = cc * lax.rsqrt(var + EPS) * lng_ref[...] + lnb_ref[...]
    o_conv = ln * _sigmoid(ln)

    o = jnp.concatenate([o_lru, o_conv], axis=-1).astype(BF16)
    h_ref[...] = x_ref[...] + jnp.dot(o, wout_ref[...], preferred_element_type=F32)


def _mixer(u, x2, cw, cb, gwa, gwx, ba, bx, lam, dww, dwb, lng, lnb, wout, batch, seq, ts):
    n_s = seq // ts
    tok = lambda b, s: (b * n_s + s, 0)
    const2 = lambda b, s: (0, 0)
    const3 = lambda b, s: (0, 0, 0)
    row_spec = pl.BlockSpec((1, D_LRU), const2)
    return pl.pallas_call(
        functools.partial(_mixer_kernel, ts=ts),
        grid=(batch, n_s),
        in_specs=[
            pl.BlockSpec((ts, D_IN), tok),
            pl.BlockSpec((ts, D_MODEL), tok),
            pl.BlockSpec((LRU_CONV_WIDTH, D_LRU), const2),
            row_spec,
            pl.BlockSpec((N_GATE_GROUPS, GATE_GROUP, GATE_GROUP), const3),
            pl.BlockSpec((N_GATE_GROUPS, GATE_GROUP, GATE_GROUP), const3),
            row_spec, row_spec, row_spec,
            pl.BlockSpec((CONF_CONV_WIDTH, D_CONV), const2),
            row_spec, row_spec, row_spec,
            pl.BlockSpec((D_MODEL, D_MODEL), const2),
        ],
        out_specs=pl.BlockSpec((ts, D_MODEL), tok),
        out_shape=jax.ShapeDtypeStruct((batch * seq, D_MODEL), F32),
        scratch_shapes=[
            pltpu.VMEM((LRU_HALO + ts, D_LRU), F32),
            pltpu.VMEM((CONF_HALO + ts, D_CONV), F32),
            pltpu.VMEM((ts, D_LRU), F32),
            pltpu.VMEM((ts, D_LRU), F32),
            pltpu.VMEM((ts, D_LRU), F32),
            pltpu.VMEM((SUBLANES, D_LRU), F32),
        ],
        compiler_params=pltpu.CompilerParams(
            dimension_semantics=("arbitrary", "arbitrary"),
            vmem_limit_bytes=VMEM_LIMIT_BYTES),
        name="mixer",
    )(u, x2, cw, cb, gwa, gwx, ba, bx, lam, dww, dwb, lng, lnb, wout)


def _mlp_kernel(h_ref, g_ref, w1_ref, w2_ref, gf_ref, o_ref, hn_ref, acc_ref, *, final_norm):
    j = pl.program_id(1)

    @pl.when(j == 0)
    def _():
        hn_ref[...] = (_rms_scale(h_ref[...]) * g_ref[...]).astype(BF16)
        acc_ref[...] = jnp.zeros_like(acc_ref)

    z = jnp.dot(hn_ref[...], w1_ref[...], preferred_element_type=F32)
    z = jnp.square(jnp.maximum(z, 0.0)).astype(BF16)
    acc_ref[...] += jnp.dot(z, w2_ref[...], preferred_element_type=F32)

    @pl.when(j == pl.num_programs(1) - 1)
    def _():
        y = h_ref[...] + acc_ref[...]
        o_ref[...] = _rms_scale(y) * gf_ref[...] if final_norm else y


def _mlp(h, g, w1, w2, gf, final_norm, tm, tf):
    n_tok = h.shape[0]
    return pl.pallas_call(
        functools.partial(_mlp_kernel, final_norm=final_norm),
        grid=(n_tok // tm, D_FF // tf),
        in_specs=[
            pl.BlockSpec((tm, D_MODEL), lambda i, j: (i, 0)),
            pl.BlockSpec((1, D_MODEL), lambda i, j: (0, 0)),
            pl.BlockSpec((D_MODEL, tf), lambda i, j: (0, j)),
            pl.BlockSpec((tf, D_MODEL), lambda i, j: (j, 0)),
            pl.BlockSpec((1, D_MODEL), lambda i, j: (0, 0)),
        ],
        out_specs=pl.BlockSpec((tm, D_MODEL), lambda i, j: (i, 0)),
        out_shape=jax.ShapeDtypeStruct((n_tok, D_MODEL), F32),
        scratch_shapes=[pltpu.VMEM((tm, D_MODEL), BF16), pltpu.VMEM((tm, D_MODEL), F32)],
        compiler_params=pltpu.CompilerParams(
            dimension_semantics=("arbitrary", "arbitrary"),
            vmem_limit_bytes=VMEM_LIMIT_BYTES),
        name="mlp",
    )(h, g, w1, w2, gf)


def _block_diag_groups(w):
    heads_per_group = GATE_GROUP // LRU_HEAD_DIM
    w = w.reshape(N_GATE_GROUPS, heads_per_group, LRU_HEAD_DIM, LRU_HEAD_DIM)
    eye = jnp.eye(heads_per_group, dtype=w.dtype)
    out = jnp.einsum("ghij,hk->ghikj", w, eye)
    return out.reshape(N_GATE_GROUPS, GATE_GROUP, GATE_GROUP)


def kernel(x, mix_norm_g, w_in, lru_conv_w, lru_conv_b, lru_gate_a_w, lru_gate_a_b, lru_gate_x_w,
           lru_gate_x_b, lru_lambda, conf_dw_w, conf_dw_b, conf_ln_g, conf_ln_b, w_out, mlp_norm_g,
           mlp_w1, mlp_w2, final_norm_g):
    batch, seq, _ = x.shape
    depth = w_in.shape[0]
    h = x.reshape(batch * seq, D_MODEL)
    for l in range(depth):
        u = _inproj(h, mix_norm_g[l][None], w_in[l].astype(BF16), tm=512, tn=1024)
        h = _mixer(
            u, h, lru_conv_w[l], lru_conv_b[l][None],
            _block_diag_groups(lru_gate_a_w[l]).astype(BF16),
            _block_diag_groups(lru_gate_x_w[l]).astype(BF16),
            lru_gate_a_b[l][None], lru_gate_x_b[l][None], lru_lambda[l][None],
            conf_dw_w[l], conf_dw_b[l][None], conf_ln_g[l][None], conf_ln_b[l][None],
            w_out[l].astype(BF16), batch, seq, ts=256)
        h = _mlp(h, mlp_norm_g[l][None], mlp_w1[l].astype(BF16), mlp_w2[l].astype(BF16),
                 final_norm_g[None], final_norm=(l == depth - 1), tm=512, tf=1024)
    return h.reshape(batch, seq, D_MODEL)
```
